```python
import jax
import jax.numpy as jnp
from jax import lax
import numpy as np

D_MODEL = 4096
BATCH = 1
SEQ = 8192
DEPTH = 4

POOL_WINDOWS = (2, 4, 8, 16)
POOL_GROUPS = 4
POOL_WIDTH = D_MODEL // 2
POOL_GROUP_DIM = POOL_WIDTH // POOL_GROUPS
HEAD_DIM = 128
N_Q_HEADS = D_MODEL // 256
N_KV_HEADS = N_Q_HEADS // 4
GQA_GROUP = N_Q_HEADS // N_KV_HEADS
NSA_WIDTH = N_Q_HEADS * HEAD_DIM
KV_WIDTH = N_KV_HEADS * HEAD_DIM
N_NSA_BRANCHES = 3
CMP_BLOCK = 32
CMP_STRIDE = 16
SLC_BLOCK = 64
SLC_TOPK = 16
WINDOW = 512
Q_BLOCK = 128
N_MERGE = 2
IN_WIDTHS = (POOL_WIDTH, POOL_WIDTH, NSA_WIDTH, NSA_WIDTH, KV_WIDTH, KV_WIDTH, KV_WIDTH, KV_WIDTH, KV_WIDTH, KV_WIDTH, N_NSA_BRANCHES * N_Q_HEADS, N_MERGE * D_MODEL)
IN_WIDTH = 2 * POOL_WIDTH + 2 * NSA_WIDTH + 6 * KV_WIDTH + N_NSA_BRANCHES * N_Q_HEADS + N_MERGE * D_MODEL
EPS = 1e-6
NEG_INF = -1e30
FORCED_SCORE = 1e6

kernel_name = 'hybrid_pool_nsa_gated_trunk'


def rms_norm(x, g):
    xf = x.astype(jnp.float32)
    y = xf * lax.rsqrt(jnp.mean(xf * xf, axis=-1, keepdims=True) + EPS)
    return (y * g.astype(jnp.float32)).astype(x.dtype)


def masked_softmax(s, mask):
    p = jax.nn.softmax(jnp.where(mask, s, NEG_INF), axis=-1)
    return p * mask.astype(p.dtype)


def alibi_slopes():
    h = jnp.arange(1, N_Q_HEADS + 1, dtype=jnp.float32)
    return jnp.exp2(-8.0 * h / N_Q_HEADS).reshape(N_KV_HEADS, GQA_GROUP)


def split_points():
    return np.cumsum(np.array(IN_WIDTHS))[:-1].tolist()


def pool_mixer(u, w_grp, scale):
    B, T, _ = u.shape
    uf = u.astype(jnp.float32)
    c = jnp.concatenate([jnp.zeros((B, 1, POOL_WIDTH), jnp.float32), jnp.cumsum(uf, axis=1)], axis=1)
    t = jnp.arange(T)
    outs = []
    for gi, w in enumerate(POOL_WINDOWS):
        sl = slice(gi * POOL_GROUP_DIM, (gi + 1) * POOL_GROUP_DIM)
        lo = jnp.maximum(t + 1 - w, 0)
        win_sum = c[:, 1:, sl] - c[:, lo][..., sl]
        cnt = (t + 1 - lo).astype(jnp.float32)[None, :, None]
        outs.append(win_sum / cnt - uf[..., sl])
    m = jnp.stack(outs, axis=2).astype(u.dtype)
    m = jnp.einsum('btgc,gcd->btgd', m, w_grp).reshape(B, T, POOL_WIDTH)
    return m * scale


def compress(k, pe, w1, b1, w2):
    T = k.shape[0]
    r = CMP_BLOCK // CMP_STRIDE
    n_chunks = T // CMP_STRIDE
    kr = k.reshape(n_chunks, CMP_STRIDE, N_KV_HEADS, HEAD_DIM)
    blocks = jnp.concatenate([kr[i:n_chunks - r + 1 + i] for i in range(r)], axis=1)
    blocks = blocks + pe[None, :, None, :]
    flat = jnp.transpose(blocks, (0, 2, 1, 3)).reshape(-1, N_KV_HEADS, CMP_BLOCK * HEAD_DIM)
    return jax.nn.silu(flat @ w1 + b1) @ w2


def cmp_to_slc(nc, ns):
    cs = jnp.arange(nc)[:, None] * CMP_STRIDE
    ss = jnp.arange(ns)[None, :] * SLC_BLOCK
    ov = jnp.maximum(jnp.minimum(cs + CMP_BLOCK, ss + SLC_BLOCK) - jnp.maximum(cs, ss), 0)
    return ov.astype(jnp.float32) / CMP_STRIDE


def nsa_sequence(q, kc, vc, ks, vs, kw, vw, gates):
    T = q.shape[0]
    nc = kc.shape[0]
    ns = T // SLC_BLOCK
    nb = T // Q_BLOCK
    topk = min(SLC_TOPK, ns)
    scale = HEAD_DIM ** -0.5
    slopes = alibi_slopes()
    ovl = cmp_to_slc(nc, ns)
    cmp_end = jnp.arange(nc) * CMP_STRIDE + CMP_BLOCK - 1
    ks_blk = jnp.transpose(ks.reshape(ns, SLC_BLOCK, N_KV_HEADS, HEAD_DIM), (2, 0, 1, 3))
    vs_blk = jnp.transpose(vs.reshape(ns, SLC_BLOCK, N_KV_HEADS, HEAD_DIM), (2, 0, 1, 3))
    pad = jnp.zeros((WINDOW, N_KV_HEADS, HEAD_DIM), kw.dtype)
    kw_p = jnp.concatenate([pad, kw], axis=0)
    vw_p = jnp.concatenate([pad, vw], axis=0)
    qb = q.reshape(nb, Q_BLOCK, N_KV_HEADS, GQA_GROUP, HEAD_DIM)
    gb = gates.reshape(nb, Q_BLOCK, N_KV_HEADS, GQA_GROUP, N_NSA_BRANCHES)

    def block(args):
        b, qi, gi = args
        t = b * Q_BLOCK + jnp.arange(Q_BLOCK)
        s = jnp.einsum('qhgd,nhd->qhgn', qi, kc).astype(jnp.float32) * scale
        dist = (t[:, None] - cmp_end[None, :]).astype(jnp.float32)
        s = s - slopes[None, :, :, None] * dist[:, None, None, :]
        mask_c = (cmp_end[None, :] <= t[:, None])[:, None, None, :]
        p_c = masked_softmax(s, mask_c)
        o_c = jnp.einsum('qhgn,nhd->qhgd', p_c.astype(vc.dtype), vc)
        imp = jnp.einsum('qhn,nj->qhj', p_c.sum(axis=2), ovl)
        j = jnp.arange(ns)[None, :]
        cur = (t // SLC_BLOCK)[:, None]
        valid = j <= cur
        forced = (j == 0) | (j == cur) | (j == cur - 1)
        score = jnp.where(valid[:, None, :], jnp.where(forced[:, None, :], FORCED_SCORE, imp), -1.0)
        vals, idx = lax.top_k(score, topk)
        sel_ok = vals >= 0.0
        idx_h = jnp.transpose(idx, (1, 0, 2))
        k_sel = jax.vmap(lambda kb, ib: kb[ib])(ks_blk, idx_h)
        v_sel = jax.vmap(lambda vb, ib: vb[ib])(vs_blk, idx_h)
        pos = idx[..., None] * SLC_BLOCK + jnp.arange(SLC_BLOCK)
        s = jnp.einsum('qhgd,hqnld->qhgnl', qi, k_sel).astype(jnp.float32) * scale
        dist = (t[:, None, None, None] - pos).astype(jnp.float32)
        s = s - slopes[None, :, :, None, None] * dist[:, :, None]
        mask_s = (sel_ok[..., None] & (pos <= t[:, None, None, None]))[:, :, None]
        s = s.reshape(Q_BLOCK, N_KV_HEADS, GQA_GROUP, topk * SLC_BLOCK)
        mask_s = mask_s.reshape(Q_BLOCK, N_KV_HEADS, 1, topk * SLC_BLOCK)
        p_s = masked_softmax(s, mask_s)
        v_flat = v_sel.reshape(N_KV_HEADS, Q_BLOCK, topk * SLC_BLOCK, HEAD_DIM)
        o_s = jnp.einsum('qhgm,hqmd->qhgd', p_s.astype(v_flat.dtype), v_flat)
        kwin = lax.dynamic_slice_in_dim(kw_p, b * Q_BLOCK, Q_BLOCK + WINDOW, axis=0)
        vwin = lax.dynamic_slice_in_dim(vw_p, b * Q_BLOCK, Q_BLOCK + WINDOW, axis=0)
        spos = b * Q_BLOCK - WINDOW + jnp.arange(Q_BLOCK + WINDOW)
        s = jnp.einsum('qhgd,khd->qhgk', qi, kwin).astype(jnp.float32) * scale
        dist = t[:, None] - spos[None, :]
        s = s - slopes[None, :, :, None] * dist.astype(jnp.float32)[:, None, None, :]
        mask_w = ((dist >= 0) & (dist < WINDOW) & (spos[None, :] >= 0))[:, None, None, :]
        p_w = masked_softmax(s, mask_w)
        o_w = jnp.einsum('qhgk,khd->qhgd', p_w.astype(vwin.dtype), vwin)
        g = jax.nn.sigmoid(gi.astype(jnp.float32))
        o = g[..., 0:1] * o_c + g[..., 1:2] * o_s + g[..., 2:3] * o_w
        return o.astype(q.dtype)

    out = lax.map(block, (jnp.arange(nb), qb, gb))
    return out.reshape(T, NSA_WIDTH)


def hybrid_layer(x, norm_g, w_in, merge_bias, pool_w, pool_scale, q_norm_g, k_norm_g, cmp_pe, cmp_w1, cmp_b1, cmp_w2, w_pool_up, w_nsa_up, w_out):
    B, T, _ = x.shape
    h = rms_norm(x, norm_g)
    proj = h @ w_in
    pool_u, pool_z, q, nsa_z, kc, vc, ks, vs, kw, vw, nsa_g, merge_g = jnp.split(proj, split_points(), axis=-1)
    pool_out = pool_mixer(pool_u, pool_w, pool_scale) * jax.nn.silu(pool_z)
    heads = lambda a: a.reshape(B, T, N_KV_HEADS, HEAD_DIM)
    q = rms_norm(q.reshape(B, T, N_Q_HEADS, HEAD_DIM), q_norm_g)
    comp = jax.vmap(compress, in_axes=(0, None, None, None, None))
    kc = rms_norm(comp(heads(kc), cmp_pe[0], cmp_w1[0], cmp_b1[0], cmp_w2[0]), k_norm_g[0])
    vc = comp(heads(vc), cmp_pe[1], cmp_w1[1], cmp_b1[1], cmp_w2[1])
    ks = rms_norm(heads(ks), k_norm_g[1])
    kw = rms_norm(heads(kw), k_norm_g[2])
    gates = nsa_g.reshape(B, T, N_Q_HEADS, N_NSA_BRANCHES)
    o = jax.vmap(nsa_sequence)(q, kc, vc, ks, heads(vs), kw, heads(vw), gates)
    nsa_out = o * jax.nn.silu(nsa_z)
    gm = jax.nn.sigmoid(merge_g.reshape(B, T, N_MERGE, D_MODEL) + merge_bias)
    y = gm[:, :, 0] * (pool_out @ w_pool_up) + gm[:, :, 1] * (nsa_out @ w_nsa_up)
    return x + (y.astype(x.dtype) @ w_out)


def setup_inputs(seed: int = 0) -> dict:
    key = jax.random.key(seed)
    ks = jax.random.split(key, 16)

    def nrm(k, shape, s):
        return jax.random.normal(k, shape, jnp.float32) * s

    return {
        'x': nrm(ks[0], (BATCH, SEQ, D_MODEL), 1.0),
        'norm_g': 1.0 + nrm(ks[1], (DEPTH, D_MODEL), 0.02),
        'w_in': nrm(ks[2], (DEPTH, D_MODEL, IN_WIDTH), D_MODEL ** -0.5),
        'merge_bias': nrm(ks[3], (DEPTH, N_MERGE, D_MODEL), 0.01),
        'pool_w': nrm(ks[4], (DEPTH, POOL_GROUPS, POOL_GROUP_DIM, POOL_GROUP_DIM), POOL_GROUP_DIM ** -0.5),
        'pool_scale': 1.0 + nrm(ks[5], (DEPTH, POOL_WIDTH), 0.1),
        'q_norm_g': 1.0 + nrm(ks[6], (DEPTH, HEAD_DIM), 0.02),
        'k_norm_g': 1.0 + nrm(ks[7], (DEPTH, N_NSA_BRANCHES, HEAD_DIM), 0.02),
        'cmp_pe': nrm(ks[8], (DEPTH, 2, CMP_BLOCK, HEAD_DIM), 0.1),
        'cmp_w1': nrm(ks[9], (DEPTH, 2, CMP_BLOCK * HEAD_DIM, HEAD_DIM), (CMP_BLOCK * HEAD_DIM) ** -0.5),
        'cmp_b1': nrm(ks[10], (DEPTH, 2, HEAD_DIM), 0.01),
        'cmp_w2': nrm(ks[11], (DEPTH, 2, HEAD_DIM, HEAD_DIM), HEAD_DIM ** -0.5),
        'w_pool_up': nrm(ks[12], (DEPTH, POOL_WIDTH, D_MODEL), POOL_WIDTH ** -0.5),
        'w_nsa_up': nrm(ks[13], (DEPTH, NSA_WIDTH, D_MODEL), NSA_WIDTH ** -0.5),
        'w_out': nrm(ks[14], (DEPTH, D_MODEL, D_MODEL), D_MODEL ** -0.5),
    }


def reference(x, norm_g, w_in, merge_bias, pool_w, pool_scale, q_norm_g, k_norm_g, cmp_pe, cmp_w1, cmp_b1, cmp_w2, w_pool_up, w_nsa_up, w_out):
    for l in range(DEPTH):
        x = hybrid_layer(x, norm_g[l], w_in[l], merge_bias[l], pool_w[l], pool_scale[l], q_norm_g[l], k_norm_g[l], cmp_pe[l], cmp_w1[l], cmp_b1[l], cmp_w2[l], w_pool_up[l], w_nsa_up[l], w_out[l])
    return x
```

```python
import functools

import jax
import jax.numpy as jnp
import numpy as np
from jax import lax
from jax.experimental import pallas as pl
from jax.experimental.pallas import tpu as pltpu

D_MODEL = 4096
DEPTH = 4
POOL_WINDOWS = (2, 4, 8, 16)
POOL_WIDTH = D_MODEL // 2
POOL_GROUP_DIM = POOL_WIDTH // len(POOL_WINDOWS)
POOL_HALO = 16
HEAD_DIM = 128
N_Q_HEADS = D_MODEL // 256
N_KV_HEADS = N_Q_HEADS // 4
GQA = N_Q_HEADS // N_KV_HEADS
NSA_WIDTH = N_Q_HEADS * HEAD_DIM
KV_WIDTH = N_KV_HEADS * HEAD_DIM
N_BRANCH = 3
CMP_BLOCK = 32
CMP_STRIDE = 16
SLC_BLOCK = 64
SLC_TOPK = 16
WINDOW = 512
Q_BLOCK = 128
EPS = 1e-6
NEG_INF = -1e30
FORCED_SCORE = 1e6
SM_SCALE = HEAD_DIM ** -0.5

OFF_Q = 2 * POOL_WIDTH
OFF_NSA_Z = OFF_Q + NSA_WIDTH
OFF_KV = OFF_NSA_Z + NSA_WIDTH
OFF_GATE = OFF_KV + 6 * KV_WIDTH
OFF_MERGE = OFF_GATE + N_BRANCH * N_Q_HEADS
GROUP_ROWS = GQA * Q_BLOCK
SEL_LANES = 128
SEL_TILE = 512

F32 = jnp.float32
BF16 = jnp.bfloat16
MIB = 1024 * 1024


def _params(sem, vmem_mib):
    return pltpu.CompilerParams(dimension_semantics=sem, vmem_limit_bytes=vmem_mib * MIB)


def _dot(a, b):
    return jnp.dot(a, b, preferred_element_type=F32)


def _dot_nt(a, b):
    return lax.dot_general(a, b, (((1,), (1,)), ((), ())), preferred_element_type=F32)


def _silu(x):
    return x * jax.nn.sigmoid(x)


def _norm_kernel(x_ref, g_ref, o_ref):
    x = x_ref[...]
    ms = jnp.mean(x * x, axis=-1, keepdims=True)
    o_ref[...] = (x * lax.rsqrt(ms + EPS) * g_ref[...]).astype(o_ref.dtype)


def _rmsnorm(x, g, tm=512):
    t, d = x.shape
    return pl.pallas_call(
        _norm_kernel,
        grid=(t // tm,),
        in_specs=[pl.BlockSpec((tm, d), lambda i: (i, 0)), pl.BlockSpec((1, d), lambda i: (0, 0))],
        out_specs=pl.BlockSpec((tm, d), lambda i: (i, 0)),
        out_shape=jax.ShapeDtypeStruct((t, d), BF16),
        compiler_params=_params(("parallel",), 40),
    )(x, g.reshape(1, d))


def _mm_kernel(a_ref, b_ref, o_ref):
    o_ref[...] = _dot(a_ref[...], b_ref[...]).astype(o_ref.dtype)


def _matmul(a, b, out_dtype, tm, tn):
    m, k = a.shape
    n = b.shape[1]
    return pl.pallas_call(
        _mm_kernel,
        grid=(m // tm, n // tn),
        in_specs=[pl.BlockSpec((tm, k), lambda i, j: (i, 0)), pl.BlockSpec((k, tn), lambda i, j: (0, j))],
        out_specs=pl.BlockSpec((tm, tn), lambda i, j: (i, j)),
        out_shape=jax.ShapeDtypeStruct((m, n), out_dtype),
        compiler_params=_params(("parallel", "arbitrary"), 48),
    )(a, b)


def _up_kernel(h_ref, wm0_ref, wm1_ref, b0_ref, b1_ref, p_ref, wp_ref, n_ref, wn_ref, y_ref):
    h = h_ref[...]
    g0 = jax.nn.sigmoid(_dot(h, wm0_ref[...]) + b0_ref[...])
    g1 = jax.nn.sigmoid(_dot(h, wm1_ref[...]) + b1_ref[...])
    y = g0 * _dot(p_ref[...], wp_ref[...]) + g1 * _dot(n_ref[...], wn_ref[...])
    y_ref[...] = y.astype(y_ref.dtype)


def _merge_up(h, w_merge, merge_bias, pool_out, w_pool_up, nsa_out, w_nsa_up, tm=512, tn=512):
    t, d = h.shape
    nj = d // tn
    b = merge_bias.reshape(1, 2 * d)
    return pl.pallas_call(
        _up_kernel,
        grid=(t // tm, nj),
        in_specs=[
            pl.BlockSpec((tm, d), lambda i, j: (i, 0)),
            pl.BlockSpec((d, tn), lambda i, j: (0, j)),
            pl.BlockSpec((d, tn), lambda i, j: (0, j + nj)),
            pl.BlockSpec((1, tn), lambda i, j: (0, j)),
            pl.BlockSpec((1, tn), lambda i, j: (0, j + nj)),
            pl.BlockSpec((tm, POOL_WIDTH), lambda i, j: (i, 0)),
            pl.BlockSpec((POOL_WIDTH, tn), lambda i, j: (0, j)),
            pl.BlockSpec((tm, NSA_WIDTH), lambda i, j: (i, 0)),
            pl.BlockSpec((NSA_WIDTH, tn), lambda i, j: (0, j)),
        ],
        out_specs=pl.BlockSpec((tm, tn), lambda i, j: (i, j)),
        out_shape=jax.ShapeDtypeStruct((t, d), BF16),
        compiler_params=_params(("parallel", "arbitrary"), 52),
    )(h, w_merge, w_merge, b, b, pool_out, w_pool_up, nsa_out, w_nsa_up)


def _out_kernel(y_ref, w_ref, x_ref, o_ref):
    o_ref[...] = x_ref[...] + _dot(y_ref[...], w_ref[...])


def _out_proj(y, w_out, x, tm=1024, tn=512):
    t, d = x.shape
    return pl.pallas_call(
        _out_kernel,
        grid=(t // tm, d // tn),
        in_specs=[
            pl.BlockSpec((tm, d), lambda i, j: (i, 0)),
            pl.BlockSpec((d, tn), lambda i, j: (0, j)),
            pl.BlockSpec((tm, tn), lambda i, j: (i, j)),
        ],
        out_specs=pl.BlockSpec((tm, tn), lambda i, j: (i, j)),
        out_shape=jax.ShapeDtypeStruct((t, d), F32),
        compiler_params=_params(("parallel", "arbitrary"), 48),
    )(y, w_out, x)


def _pool_kernel(u_ref, z_ref, w_ref, sc_ref, o_ref, tail_ref, *, tt):
    i = pl.program_id(0)

    @pl.when(i == 0)
    def _():
        tail_ref[...] = jnp.zeros_like(tail_ref)

    row = i * tt + lax.broadcasted_iota(jnp.int32, (tt, 1), 0)
    for gi, w in enumerate(POOL_WINDOWS):
        sl = slice(gi * POOL_GROUP_DIM, (gi + 1) * POOL_GROUP_DIM)
        ug = u_ref[:, sl]
        s = jnp.concatenate([tail_ref[:, sl], ug], axis=0)
        sh = 1
        while sh < w:
            s = s + pltpu.roll(s, sh, axis=0)
            sh *= 2
        cnt = jnp.minimum(row + 1, w).astype(F32)
        m = (s[POOL_HALO:] / cnt - ug).astype(BF16)
        r = _dot(m, w_ref[gi])
        o_ref[:, sl] = (r * sc_ref[:, sl] * _silu(z_ref[:, sl])).astype(o_ref.dtype)
    tail_ref[...] = u_ref[tt - POOL_HALO:, :]


def _pool_branch(proj, pool_w, pool_scale, tt=512):
    t = proj.shape[0]
    return pl.pallas_call(
        functools.partial(_pool_kernel, tt=tt),
        grid=(t // tt,),
        in_specs=[
            pl.BlockSpec((tt, POOL_WIDTH), lambda i: (i, 0)),
            pl.BlockSpec((tt, POOL_WIDTH), lambda i: (i, 1)),
            pl.BlockSpec((len(POOL_WINDOWS), POOL_GROUP_DIM, POOL_GROUP_DIM), lambda i: (0, 0, 0)),
            pl.BlockSpec((1, POOL_WIDTH), lambda i: (0, 0)),
        ],
        out_specs=pl.BlockSpec((tt, POOL_WIDTH), lambda i: (i, 0)),
        out_shape=jax.ShapeDtypeStruct((t, POOL_WIDTH), BF16),
        scratch_shapes=[pltpu.VMEM((POOL_HALO, POOL_WIDTH), F32)],
        compiler_params=_params(("arbitrary",), 40),
    )(proj, proj, pool_w, pool_scale.reshape(1, POOL_WIDTH))


def _head_norm(x, g):
    ms = jnp.mean(x * x, axis=-1, keepdims=True)
    return x * lax.rsqrt(ms + EPS) * g


def _prep_kernel(q_ref, ks_ref, vs_ref, kw_ref, vw_ref, qg_ref, kg_ref, qo_ref, kso_ref, vso_ref, kwo_ref, vwo_ref):
    for hq in range(N_Q_HEADS):
        sl = slice(hq * HEAD_DIM, (hq + 1) * HEAD_DIM)
        qo_ref[:, sl] = (_head_norm(q_ref[:, sl], qg_ref[...]) * SM_SCALE).astype(BF16)
    for hk in range(N_KV_HEADS):
        sl = slice(hk * HEAD_DIM, (hk + 1) * HEAD_DIM)
        kso_ref[:, sl] = _head_norm(ks_ref[:, sl], kg_ref[1:2, :]).astype(BF16)
        kwo_ref[:, sl] = _head_norm(kw_ref[:, sl], kg_ref[2:3, :]).astype(BF16)
    vso_ref[...] = vs_ref[...].astype(BF16)
    vwo_ref[...] = vw_ref[...].astype(BF16)


def _prep(proj, q_norm_g, k_norm_g, tt=512):
    t = proj.shape[0]
    kv0 = OFF_KV // KV_WIDTH
    kv_spec = lambda c: pl.BlockSpec((tt, KV_WIDTH), lambda i: (i, kv0 + c))
    kv_out = pl.BlockSpec((tt, KV_WIDTH), lambda i: (i, 0))
    kv_shape = jax.ShapeDtypeStruct((t, KV_WIDTH), BF16)
    return pl.pallas_call(
        _prep_kernel,
        grid=(t // tt,),
        in_specs=[
            pl.BlockSpec((tt, NSA_WIDTH), lambda i: (i, OFF_Q // NSA_WIDTH)),
            kv_spec(2), kv_spec(3), kv_spec(4), kv_spec(5),
            pl.BlockSpec((1, HEAD_DIM), lambda i: (0, 0)),
            pl.BlockSpec((N_BRANCH, HEAD_DIM), lambda i: (0, 0)),
        ],
        out_specs=[pl.BlockSpec((tt, NSA_WIDTH), lambda i: (i, 0)), kv_out, kv_out, kv_out, kv_out],
        out_shape=[jax.ShapeDtypeStruct((t, NSA_WIDTH), BF16), kv_shape, kv_shape, kv_shape, kv_shape],
        compiler_params=_params(("parallel",), 40),
    )(proj, proj, proj, proj, proj, q_norm_g.reshape(1, HEAD_DIM), k_norm_g)


def _compress_kernel(x_ref, pe_ref, w1_ref, b1_ref, w2_ref, g_ref, o_ref, *, n_chunks):
    c = pl.program_id(0)
    x = x_ref[0, 0]
    lo = _dot((x + pe_ref[0, 0:1, :]).astype(BF16), w1_ref[0, 0])
    hi = _dot((x + pe_ref[0, 1:2, :]).astype(BF16), w1_ref[0, 1])
    pre = lo + pltpu.roll(hi, n_chunks - 1, axis=0) + b1_ref[0]
    out = _dot(_silu(pre).astype(BF16), w2_ref[0])
    out = jnp.where(c == 0, _head_norm(out, g_ref[...]), out)
    row = lax.broadcasted_iota(jnp.int32, (n_chunks, 1), 0)
    o_ref[0, 0] = jnp.where(row < n_chunks - 1, out, 0.0).astype(o_ref.dtype)


def _compress(proj, cmp_pe, cmp_w1, cmp_b1, cmp_w2, kc_norm_g):
    t = proj.shape[0]
    n_chunks = t // CMP_STRIDE
    cw = CMP_STRIDE * HEAD_DIM
    raw = proj[:, OFF_KV:OFF_KV + 2 * KV_WIDTH].reshape(n_chunks, CMP_STRIDE, 2, N_KV_HEADS, HEAD_DIM)
    raw = raw.transpose(2, 3, 0, 1, 4).reshape(2, N_KV_HEADS, n_chunks, cw)
    return pl.pallas_call(
        functools.partial(_compress_kernel, n_chunks=n_chunks),
        grid=(2, N_KV_HEADS),
        in_specs=[
            pl.BlockSpec((1, 1, n_chunks, cw), lambda c, h: (c, h, 0, 0)),
            pl.BlockSpec((1, 2, cw), lambda c, h: (c, 0, 0)),
            pl.BlockSpec((1, 2, cw, HEAD_DIM), lambda c, h: (c, 0, 0, 0)),
            pl.BlockSpec((1, 1, HEAD_DIM), lambda c, h: (c, 0, 0)),
            pl.BlockSpec((1, HEAD_DIM, HEAD_DIM), lambda c, h: (c, 0, 0)),
            pl.BlockSpec((1, HEAD_DIM), lambda c, h: (0, 0)),
        ],
        out_specs=pl.BlockSpec((1, 1, n_chunks, HEAD_DIM), lambda c, h: (c, h, 0, 0)),
        out_shape=jax.ShapeDtypeStruct((2, N_KV_HEADS, n_chunks, HEAD_DIM), BF16),
        compiler_params=_params(("parallel", "parallel"), 40),
    )(
        raw,
        cmp_pe.reshape(2, 2, cw),
        cmp_w1.astype(BF16).reshape(2, 2, cw, HEAD_DIM),
        cmp_b1.reshape(2, 1, HEAD_DIM),
        cmp_w2.astype(BF16),
        kc_norm_g.reshape(1, HEAD_DIM),
    )


def _stack_heads(x):
    return jnp.concatenate([x[:, g * HEAD_DIM:(g + 1) * HEAD_DIM] for g in range(GQA)], axis=0)


def _mask_rows(s, mask):
    return jnp.concatenate(
        [jnp.where(mask, s[g * Q_BLOCK:(g + 1) * Q_BLOCK], NEG_INF) for g in range(GQA)], axis=0)


def _nsa_kernel(q_ref, kc_ref, vc_ref, ks_ref, vs_ref, kw_ref, vw_ref, gate_ref, z_ref, slope_ref,
                ovl_ref, exp_ref, o_ref, m_ref, l_ref, acc_ref):
    b = pl.program_id(1)
    t0 = b * Q_BLOCK
    q4 = _stack_heads(q_ref[...])
    slope = slope_ref[0]
    t_q = t0 + lax.broadcasted_iota(jnp.int32, (Q_BLOCK, 1), 0)

    def alibi(pos):
        return slope * (pos - t0).astype(F32)

    ncp = kc_ref.shape[2]
    cend = lax.broadcasted_iota(jnp.int32, (1, ncp), 1) * CMP_STRIDE + (CMP_BLOCK - 1)
    mask_c = cend <= t_q
    s = _mask_rows(_dot_nt(q4, kc_ref[0, 0]) + alibi(cend), mask_c)
    m = jnp.max(s, axis=1, keepdims=True)
    e = jnp.exp(s - m)
    e = jnp.concatenate(
        [jnp.where(mask_c, e[g * Q_BLOCK:(g + 1) * Q_BLOCK], 0.0) for g in range(GQA)], axis=0)
    l = jnp.sum(e, axis=1, keepdims=True)
    p = e * (1.0 / jnp.where(l > 0.0, l, 1.0))
    o_c = _dot(p.astype(BF16), vc_ref[0, 0])

    psum = p[0:Q_BLOCK]
    for g in range(1, GQA):
        psum = psum + p[g * Q_BLOCK:(g + 1) * Q_BLOCK]
    p_hi = psum.astype(BF16)
    p_lo = (psum - p_hi.astype(F32)).astype(BF16)
    imp = _dot(p_hi, ovl_ref[...]) + _dot(p_lo, ovl_ref[...])
    j = lax.broadcasted_iota(jnp.int32, (Q_BLOCK, SEL_LANES), 1)
    cur = t_q // SLC_BLOCK
    valid = j <= cur
    forced = (j == 0) | (j == cur) | (j == cur - 1)
    score = jnp.where(valid, jnp.where(forced, FORCED_SCORE, imp), -1.0)
    sel = jnp.zeros((Q_BLOCK, SEL_LANES), F32)
    for _ in range(SLC_TOPK):
        hit = j == jnp.argmax(score, axis=1, keepdims=True).astype(jnp.int32)
        sel = jnp.where(hit, 1.0, sel)
        score = jnp.where(hit, -2.0, score)
    sel = jnp.where(valid, sel, 0.0).astype(BF16)

    m_ref[...] = jnp.full_like(m_ref, NEG_INF)
    l_ref[...] = jnp.zeros_like(l_ref)
    acc_ref[...] = jnp.zeros_like(acc_ref)

    def sel_tile(kt, causal):
        start = pl.multiple_of(kt * SEL_TILE, SEL_TILE)
        pos = start + lax.broadcasted_iota(jnp.int32, (1, SEL_TILE), 1)
        mask = _dot(sel, exp_ref[:, pl.ds(start, SEL_TILE)]) > 0.5
        if causal:
            mask = mask & (pos <= t_q)
        s = _mask_rows(_dot_nt(q4, ks_ref[pl.ds(start, SEL_TILE), :]) + alibi(pos), mask)
        m_old = m_ref[...]
        m_new = jnp.maximum(m_old, jnp.max(s, axis=1, keepdims=True))
        alpha = jnp.exp(m_old - m_new)
        p = jnp.exp(s - m_new)
        l_ref[...] = alpha * l_ref[...] + jnp.sum(p, axis=1, keepdims=True)
        acc_ref[...] = alpha * acc_ref[...] + _dot(p.astype(BF16), vs_ref[pl.ds(start, SEL_TILE), :])
        m_ref[...] = m_new

    n_full = b // (SEL_TILE // Q_BLOCK)

    def body(kt, carry):
        sel_tile(kt, False)
        return carry

    lax.fori_loop(0, n_full, body, 0)
    sel_tile(n_full, True)
    o_s = acc_ref[...] * (1.0 / l_ref[...])

    wlen = WINDOW + Q_BLOCK
    ws = pl.multiple_of(jnp.maximum(t0 - WINDOW, 0), Q_BLOCK)
    pos = ws + lax.broadcasted_iota(jnp.int32, (1, wlen), 1)
    dist = t_q - pos
    mask_w = (dist >= 0) & (dist < WINDOW)
    s = _mask_rows(_dot_nt(q4, kw_ref[pl.ds(ws, wlen), :]) + alibi(pos), mask_w)
    m = jnp.max(s, axis=1, keepdims=True)
    p = jnp.exp(s - m)
    l = jnp.sum(p, axis=1, keepdims=True)
    o_w = _dot(p.astype(BF16), vw_ref[pl.ds(ws, wlen), :]) * (1.0 / l)

    gate = jax.nn.sigmoid(gate_ref[...])
    for g in range(GQA):
        rows = slice(g * Q_BLOCK, (g + 1) * Q_BLOCK)
        cols = slice(g * HEAD_DIM, (g + 1) * HEAD_DIM)
        o = (gate[:, 3 * g:3 * g + 1] * o_c[rows] + gate[:, 3 * g + 1:3 * g + 2] * o_s[rows]
             + gate[:, 3 * g + 2:3 * g + 3] * o_w[rows])
        o_ref[:, cols] = (o * _silu(z_ref[:, cols])).astype(o_ref.dtype)


def _nsa_constants(t):
    nc = t // CMP_STRIDE - 1
    ncp = t // CMP_STRIDE
    cs = np.arange(ncp)[:, None] * CMP_STRIDE
    ss = np.arange(SEL_LANES)[None, :] * SLC_BLOCK
    ov = np.maximum(np.minimum(cs + CMP_BLOCK, ss + SLC_BLOCK) - np.maximum(cs, ss), 0) / CMP_STRIDE
    ov = np.where(np.arange(ncp)[:, None] < nc, ov, 0.0)
    expand = (np.arange(t)[None, :] // SLC_BLOCK == np.arange(SEL_LANES)[:, None]).astype(np.float32)
    return jnp.asarray(ov, BF16), jnp.asarray(expand, BF16)


def _alibi_slopes():
    h = jnp.arange(1, N_Q_HEADS + 1, dtype=F32)
    s = jnp.exp2(-8.0 * h / N_Q_HEADS).reshape(N_KV_HEADS, GQA, 1, 1)
    return jnp.broadcast_to(s, (N_KV_HEADS, GQA, Q_BLOCK, 1)).reshape(N_KV_HEADS, GROUP_ROWS, 1)


def _nsa(qn, cmp_kv, ksn, vsb, kwn, vwb, gates, proj):
    t = qn.shape[0]
    assert t % SEL_TILE == 0 and t // SLC_BLOCK <= SEL_LANES and t >= WINDOW + Q_BLOCK
    ncp = t // CMP_STRIDE
    ovl, expand = _nsa_constants(t)
    gw = GQA * HEAD_DIM
    kv_spec = pl.BlockSpec((t, HEAD_DIM), lambda h, b: (0, h))
    return pl.pallas_call(
        _nsa_kernel,
        grid=(N_KV_HEADS, t // Q_BLOCK),
        in_specs=[
            pl.BlockSpec((Q_BLOCK, gw), lambda h, b: (b, h)),
            pl.BlockSpec((1, 1, ncp, HEAD_DIM), lambda h, b: (0, h, 0, 0)),
            pl.BlockSpec((1, 1, ncp, HEAD_DIM), lambda h, b: (1, h, 0, 0)),
            kv_spec, kv_spec, kv_spec, kv_spec,
            pl.BlockSpec((Q_BLOCK, HEAD_DIM), lambda h, b: (b, h)),
            pl.BlockSpec((Q_BLOCK, gw), lambda h, b: (b, OFF_NSA_Z // gw + h)),
            pl.BlockSpec((1, GROUP_ROWS, 1), lambda h, b: (h, 0, 0)),
            pl.BlockSpec((ncp, SEL_LANES), lambda h, b: (0, 0)),
            pl.BlockSpec((SEL_LANES, t), lambda h, b: (0, 0)),
        ],
        out_specs=pl.BlockSpec((Q_BLOCK, gw), lambda h, b: (b, h)),
        out_shape=jax.ShapeDtypeStruct((t, NSA_WIDTH), BF16),
        scratch_shapes=[
            pltpu.VMEM((GROUP_ROWS, 1), F32),
            pltpu.VMEM((GROUP_ROWS, 1), F32),
            pltpu.VMEM((GROUP_ROWS, HEAD_DIM), F32),
        ],
        compiler_params=_params(("parallel", "arbitrary"), 48),
    )(qn, cmp_kv, cmp_kv, ksn, vsb, kwn, vwb, gates, proj, _alibi_slopes(), ovl, expand)


def _gate_weight(w_in):
    wg = w_in[:, OFF_GATE:OFF_MERGE].reshape(D_MODEL, N_KV_HEADS, GQA * N_BRANCH)
    wg = jnp.pad(wg, ((0, 0), (0, 0), (0, HEAD_DIM - GQA * N_BRANCH)))
    return wg.reshape(D_MODEL, N_KV_HEADS * HEAD_DIM).astype(BF16)


def _layer(x, norm_g, w_in, merge_bias, pool_w, pool_scale, q_norm_g, k_norm_g, cmp_pe, cmp_w1, cmp_b1,
           cmp_w2, w_pool_up, w_nsa_up, w_out):
    h = _rmsnorm(x, norm_g)
    proj = _matmul(h, w_in[:, :OFF_GATE].astype(BF16), F32, tm=1024, tn=512)
    gates = _matmul(h, _gate_weight(w_in), F32, tm=1024, tn=N_KV_HEADS * HEAD_DIM)
    pool_out = _pool_branch(proj, pool_w.astype(BF16), pool_scale)
    qn, ksn, vsb, kwn, vwb = _prep(proj, q_norm_g, k_norm_g)
    cmp_kv = _compress(proj, cmp_pe, cmp_w1, cmp_b1, cmp_w2, k_norm_g[0])
    nsa_out = _nsa(qn, cmp_kv, ksn, vsb, kwn, vwb, gates, proj)
    y = _merge_up(h, w_in[:, OFF_MERGE:].astype(BF16), merge_bias, pool_out, w_pool_up.astype(BF16),
                  nsa_out, w_nsa_up.astype(BF16))
    return _out_proj(y, w_out.astype(BF16), x)


def kernel(x, norm_g, w_in, merge_bias, pool_w, pool_scale, q_norm_g, k_norm_g, cmp_pe, cmp_w1, cmp_b1, cmp_w2,
           w_pool_up, w_nsa_up, w_out):
    b, t, d = x.shape
    outs = []
    for bi in range(b):
        xb = x[bi]
        for l in range(norm_g.shape[0]):
            xb = _layer(xb, norm_g[l], w_in[l], merge_bias[l], pool_w[l], pool_scale[l], q_norm_g[l], k_norm_g[l],
                        cmp_pe[l], cmp_w1[l], cmp_b1[l], cmp_w2[l], w_pool_up[l], w_nsa_up[l], w_out[l])
        outs.append(xb)
    return outs[0][None] if b == 1 else jnp.stack(outs, axis=0)
```
